```python
import math
import jax, jax.numpy as jnp
from jax import lax
import numpy as np

D_MODEL = 1024
BATCH = 8
SEQ = 8192
DEPTH = 1

SSM_WIDTH = D_MODEL // 2
SSM_GROUP = 16
SSM_GROUPS = SSM_WIDTH // SSM_GROUP
SSM_STATE = 64
CONV_WIDTH = D_MODEL // 2
CONV_KERNEL = 31
FFN_HIDDEN = ((8 * D_MODEL // 3 + 255) // 256) * 256
FFN_KERNEL = 3
N_COND = 6
IN_PROJ_WIDTH = SSM_WIDTH + 2 * CONV_WIDTH + 2 * D_MODEL
DEEPNORM_ALPHA = (2.0 * DEPTH) ** 0.25
DEEPNORM_BETA = (8.0 * DEPTH) ** -0.25
LN_EPS = 1e-5
DT_MIN = 1e-3
DT_MAX = 1e-1

kernel_name = "hybrid_s5_conformer_convffn_deepnorm_adaln"


def _layer_norm_plain(x):
    xf = x.astype(jnp.float32)
    mu = jnp.mean(xf, axis=-1, keepdims=True)
    var = jnp.mean(jnp.square(xf - mu), axis=-1, keepdims=True)
    return ((xf - mu) * lax.rsqrt(var + LN_EPS)).astype(x.dtype)


def _layer_norm_affine(x, g, b):
    xf = x.astype(jnp.float32)
    mu = jnp.mean(xf, axis=-1, keepdims=True)
    var = jnp.mean(jnp.square(xf - mu), axis=-1, keepdims=True)
    y = (xf - mu) * lax.rsqrt(var + LN_EPS) * g.astype(jnp.float32) + b.astype(jnp.float32)
    return y.astype(x.dtype)


def _modulate(h, shift, scale):
    return h * (1 + scale[:, None, :]) + shift[:, None, :]


def _causal_depthwise_conv(x, w, b):
    k = w.shape[0]
    y = lax.conv_general_dilated(
        x, w.astype(x.dtype), window_strides=(1,), padding=[(k - 1, 0)],
        dimension_numbers=("NWC", "WIO", "NWC"), feature_group_count=x.shape[-1])
    return y + b


def _complex_affine_combine(e1, e2):
    a1r, a1i, b1r, b1i = e1
    a2r, a2i, b2r, b2i = e2
    ar = a2r * a1r - a2i * a1i
    ai = a2r * a1i + a2i * a1r
    br = a2r * b1r - a2i * b1i + b2r
    bi = a2r * b1i + a2i * b1r + b2i
    return ar, ai, br, bi


def _s5_branch(u, lam_re, lam_im, log_dt, b_re, b_im, c_re, c_im, d, w_a, w_b):
    bsz, seq = u.shape[0], u.shape[1]
    uf = u.astype(jnp.float32).reshape(bsz, seq, SSM_GROUPS, SSM_GROUP)
    lr = jnp.minimum(lam_re.astype(jnp.float32), -1e-4)
    li = lam_im.astype(jnp.float32)
    dt = jnp.exp(log_dt.astype(jnp.float32))[:, None]
    mag = jnp.exp(lr * dt)
    ang = li * dt
    lbr, lbi = mag * jnp.cos(ang), mag * jnp.sin(ang)
    num_r, num_i = lbr - 1.0, lbi
    den = lr * lr + li * li
    coef_r = (num_r * lr + num_i * li) / den
    coef_i = (num_i * lr - num_r * li) / den
    br, bi = b_re.astype(jnp.float32), b_im.astype(jnp.float32)
    bbar_r = coef_r[..., None] * br - coef_i[..., None] * bi
    bbar_i = coef_r[..., None] * bi + coef_i[..., None] * br
    bu_r = jnp.einsum("bsgp,gnp->bsgn", uf, bbar_r)
    bu_i = jnp.einsum("bsgp,gnp->bsgn", uf, bbar_i)
    a_r = jnp.broadcast_to(lbr, (1, seq, SSM_GROUPS, SSM_STATE))
    a_i = jnp.broadcast_to(lbi, (1, seq, SSM_GROUPS, SSM_STATE))
    _, _, h_r, h_i = lax.associative_scan(_complex_affine_combine, (a_r, a_i, bu_r, bu_i), axis=1)
    y = (jnp.einsum("bsgn,gpn->bsgp", h_r, c_re.astype(jnp.float32))
         - jnp.einsum("bsgn,gpn->bsgp", h_i, c_im.astype(jnp.float32))
         + d.astype(jnp.float32) * uf)
    y = jax.nn.gelu(y.reshape(bsz, seq, SSM_WIDTH), approximate=False).astype(u.dtype)
    return (y @ w_a) * jax.nn.sigmoid(y @ w_b)


def _conformer_branch(a, g, dw_w, dw_b, ln_g, ln_b, w_pw):
    v = a * jax.nn.sigmoid(g)
    v = _causal_depthwise_conv(v, dw_w, dw_b)
    v = _layer_norm_affine(v, ln_g, ln_b)
    return jax.nn.silu(v) @ w_pw


def _token_mixer(h, w_in, b_in, lam_re, lam_im, log_dt, b_re, b_im, c_re, c_im, d,
                 glu_w_a, glu_w_b, cv_dw_w, cv_dw_b, cv_ln_g, cv_ln_b, cv_w_pw, w_out):
    p = h @ w_in + b_in
    o1 = SSM_WIDTH
    o2 = o1 + CONV_WIDTH
    o3 = o2 + CONV_WIDTH
    o4 = o3 + D_MODEL
    u_ssm, cv_a, cv_g, gate_ssm, gate_cv = p[..., :o1], p[..., o1:o2], p[..., o2:o3], p[..., o3:o4], p[..., o4:]
    y_ssm = _s5_branch(u_ssm, lam_re, lam_im, log_dt, b_re, b_im, c_re, c_im, d, glu_w_a, glu_w_b)
    y_cv = _conformer_branch(cv_a, cv_g, cv_dw_w, cv_dw_b, cv_ln_g, cv_ln_b, cv_w_pw)
    merged = jax.nn.sigmoid(gate_ssm) * y_ssm + jax.nn.sigmoid(gate_cv) * y_cv
    return merged @ w_out


def _conv_ffn(h, w_up, dw_w, dw_b, w_down):
    up = _causal_depthwise_conv(h @ w_up, dw_w, dw_b)
    a, v = up[..., :FFN_HIDDEN], up[..., FFN_HIDDEN:]
    return (jax.nn.gelu(a, approximate=False) * v) @ w_down


def setup_inputs(seed: int = 0) -> dict:
    key = jax.random.key(seed)
    ks = jax.random.split(key, 32)
    L = DEPTH
    f32 = jnp.float32

    def nrm(k, shape, scale):
        return jax.random.normal(k, shape, f32) * scale

    lam_im_init = jnp.pi * jnp.arange(SSM_STATE, dtype=f32)
    return {
        "x": nrm(ks[0], (BATCH, SEQ, D_MODEL), 1.0),
        "c": nrm(ks[1], (BATCH, D_MODEL), 1.0),
        "w_cond": nrm(ks[2], (L, D_MODEL, N_COND * D_MODEL), D_MODEL ** -0.5),
        "b_cond": nrm(ks[3], (L, N_COND * D_MODEL), 0.02),
        "w_in": nrm(ks[4], (L, D_MODEL, IN_PROJ_WIDTH), D_MODEL ** -0.5),
        "b_in": nrm(ks[5], (L, IN_PROJ_WIDTH), 0.02),
        "ssm_lambda_re": -0.5 + nrm(ks[6], (L, SSM_GROUPS, SSM_STATE), 0.01),
        "ssm_lambda_im": lam_im_init + nrm(ks[7], (L, SSM_GROUPS, SSM_STATE), 0.01),
        "ssm_log_dt": jax.random.uniform(ks[8], (L, SSM_GROUPS), f32, math.log(DT_MIN), math.log(DT_MAX)),
        "ssm_b_re": nrm(ks[9], (L, SSM_GROUPS, SSM_STATE, SSM_GROUP), (2 * SSM_GROUP) ** -0.5),
        "ssm_b_im": nrm(ks[10], (L, SSM_GROUPS, SSM_STATE, SSM_GROUP), (2 * SSM_GROUP) ** -0.5),
        "ssm_c_re": nrm(ks[11], (L, SSM_GROUPS, SSM_GROUP, SSM_STATE), (2 * SSM_STATE) ** -0.5),
        "ssm_c_im": nrm(ks[12], (L, SSM_GROUPS, SSM_GROUP, SSM_STATE), (2 * SSM_STATE) ** -0.5),
        "ssm_d": nrm(ks[13], (L, SSM_GROUPS, SSM_GROUP), 1.0),
        "ssm_glu_w_a": nrm(ks[14], (L, SSM_WIDTH, D_MODEL), SSM_WIDTH ** -0.5),
        "ssm_glu_w_b": nrm(ks[15], (L, SSM_WIDTH, D_MODEL), SSM_WIDTH ** -0.5),
        "cv_dw_w": nrm(ks[16], (L, CONV_KERNEL, 1, CONV_WIDTH), CONV_KERNEL ** -0.5),
        "cv_dw_b": nrm(ks[17], (L, CONV_WIDTH), 0.02),
        "cv_ln_g": 1.0 + nrm(ks[18], (L, CONV_WIDTH), 0.02),
        "cv_ln_b": nrm(ks[19], (L, CONV_WIDTH), 0.02),
        "cv_w_pw": nrm(ks[20], (L, CONV_WIDTH, D_MODEL), CONV_WIDTH ** -0.5),
        "w_out": nrm(ks[21], (L, D_MODEL, D_MODEL), D_MODEL ** -0.5 * DEEPNORM_BETA),
        "ln1_g": 1.0 + nrm(ks[22], (L, D_MODEL), 0.02),
        "ln1_b": nrm(ks[23], (L, D_MODEL), 0.02),
        "ffn_w_up": nrm(ks[24], (L, D_MODEL, 2 * FFN_HIDDEN), D_MODEL ** -0.5),
        "ffn_dw_w": nrm(ks[25], (L, FFN_KERNEL, 1, 2 * FFN_HIDDEN), FFN_KERNEL ** -0.5),
        "ffn_dw_b": nrm(ks[26], (L, 2 * FFN_HIDDEN), 0.02),
        "ffn_w_down": nrm(ks[27], (L, FFN_HIDDEN, D_MODEL), FFN_HIDDEN ** -0.5 * DEEPNORM_BETA),
        "ln2_g": 1.0 + nrm(ks[28], (L, D_MODEL), 0.02),
        "ln2_b": nrm(ks[29], (L, D_MODEL), 0.02),
    }


def reference(x, c, w_cond, b_cond, w_in, b_in, ssm_lambda_re, ssm_lambda_im, ssm_log_dt,
              ssm_b_re, ssm_b_im, ssm_c_re, ssm_c_im, ssm_d, ssm_glu_w_a, ssm_glu_w_b,
              cv_dw_w, cv_dw_b, cv_ln_g, cv_ln_b, cv_w_pw, w_out, ln1_g, ln1_b,
              ffn_w_up, ffn_dw_w, ffn_dw_b, ffn_w_down, ln2_g, ln2_b):
    c_act = jax.nn.silu(c)
    for l in range(DEPTH):
        mod = c_act @ w_cond[l] + b_cond[l]
        sh1, sc1, g1, sh2, sc2, g2 = jnp.split(mod, N_COND, axis=-1)
        h = _modulate(_layer_norm_plain(x), sh1, sc1)
        y = _token_mixer(h, w_in[l], b_in[l], ssm_lambda_re[l], ssm_lambda_im[l], ssm_log_dt[l],
                         ssm_b_re[l], ssm_b_im[l], ssm_c_re[l], ssm_c_im[l], ssm_d[l],
                         ssm_glu_w_a[l], ssm_glu_w_b[l], cv_dw_w[l], cv_dw_b[l],
                         cv_ln_g[l], cv_ln_b[l], cv_w_pw[l], w_out[l])
        x = _layer_norm_affine(DEEPNORM_ALPHA * x + g1[:, None, :] * y, ln1_g[l], ln1_b[l])
        h = _modulate(_layer_norm_plain(x), sh2, sc2)
        y = _conv_ffn(h, ffn_w_up[l], ffn_dw_w[l], ffn_dw_b[l], ffn_w_down[l])
        x = _layer_norm_affine(DEEPNORM_ALPHA * x + g2[:, None, :] * y, ln2_g[l], ln2_b[l])
    return x
```

```python
import functools

import jax
import jax.numpy as jnp
from jax import lax
from jax.experimental import pallas as pl
from jax.experimental.pallas import tpu as pltpu

D_MODEL = 1024
BATCH = 8
SSM_WIDTH = 512
SSM_GROUP = 16
SSM_GROUPS = 32
SSM_STATE = 64
N_STATE = SSM_GROUPS * SSM_STATE
CONV_WIDTH = 512
CONV_KERNEL = 31
FFN_HIDDEN = 2816
FFN_KERNEL = 3
N_COND = 6
IN_PROJ_WIDTH = SSM_WIDTH + 2 * CONV_WIDTH + 2 * D_MODEL
DEEPNORM_ALPHA = 2.0 ** 0.25
LN_EPS = 1e-5

V7X_SUBLANES = 8
V7X_LANES = 128
V7X_MXU_DIM = 256
V7X_VMEM_BYTES = 64 * 1024 * 1024

TIME_TILE = 64
ROWS = TIME_TILE * BATCH
SCAN_LANES = 512
FFN_CHUNK = 256
CONV_ROWS = 64
HIST31 = (CONV_KERNEL - 1) * BATCH
HIST3 = (FFN_KERNEL - 1) * BATCH
VMEM_LIMIT = V7X_VMEM_BYTES - 8 * 1024 * 1024

_bf16 = jnp.bfloat16
_f32 = jnp.float32


def _mm(a, b):
    return jnp.dot(a.astype(_bf16), b.astype(_bf16), preferred_element_type=_f32)


def _ln(x):
    mu = jnp.mean(x, axis=-1, keepdims=True)
    xc = x - mu
    var = jnp.mean(xc * xc, axis=-1, keepdims=True)
    return xc * lax.rsqrt(var + LN_EPS)


def _gelu(x):
    return 0.5 * x * (1.0 + lax.erf(x * (2.0 ** -0.5)))


def _per_batch(v, rows):
    return v[None]


def _modulated_ln(x, shift, scale):
    rows, n = x.shape
    y = _ln(x).reshape(rows // BATCH, BATCH, n)
    y = y * (1.0 + scale[None]) + shift[None]
    return y.reshape(rows, n)


def _residual_ln(x, y, gate, g, b):
    rows, n = x.shape
    z = DEEPNORM_ALPHA * x.reshape(rows // BATCH, BATCH, n) + gate[None] * y.reshape(rows // BATCH, BATCH, n)
    return _ln(z.reshape(rows, n)) * g + b


def _cond_kernel(c_ref, w_ref, b_ref, o_ref):
    c = c_ref[...]
    c_act = c * jax.nn.sigmoid(c)
    o_ref[...] = jnp.dot(c_act, w_ref[...], preferred_element_type=_f32,
                         precision=lax.Precision.HIGHEST) + b_ref[...]


def _cond(c, w, b):
    n = w.shape[1]
    bn = 1536
    return pl.pallas_call(
        _cond_kernel,
        grid=(n // bn,),
        in_specs=[pl.BlockSpec((BATCH, D_MODEL), lambda j: (0, 0)),
                  pl.BlockSpec((D_MODEL, bn), lambda j: (0, j)),
                  pl.BlockSpec((1, bn), lambda j: (0, j))],
        out_specs=pl.BlockSpec((BATCH, bn), lambda j: (0, j)),
        out_shape=jax.ShapeDtypeStruct((BATCH, n), _f32),
        name="cond",
    )(c, w, b.reshape(1, n))


def _ssm_prep_kernel(lre_ref, lim_ref, ldt_ref, bre_ref, bim_ref, ar_ref, ai_ref, bbr_ref, bbi_ref):
    lr = jnp.minimum(lre_ref[...], -1e-4)
    li = lim_ref[...]
    dt = jnp.exp(ldt_ref[...])
    mag = jnp.exp(lr * dt)
    ang = li * dt
    lbr = mag * jnp.cos(ang)
    lbi = mag * jnp.sin(ang)
    num_r = lbr - 1.0
    num_i = lbi
    den = lr * lr + li * li
    coef_r = (num_r * lr + num_i * li) / den
    coef_i = (num_i * lr - num_r * li) / den
    ar_ref[...] = lbr
    ai_ref[...] = lbi
    br = bre_ref[...]
    bi = bim_ref[...]
    bbr_ref[...] = coef_r[:, None, :] * br - coef_i[:, None, :] * bi
    bbi_ref[...] = coef_r[:, None, :] * bi + coef_i[:, None, :] * br


def _ssm_prep(lam_re, lam_im, log_dt, b_re, b_im):
    g, n, p = b_re.shape
    out_shape = (jax.ShapeDtypeStruct((g, n), _f32), jax.ShapeDtypeStruct((g, n), _f32),
                 jax.ShapeDtypeStruct((g, p, n), _f32), jax.ShapeDtypeStruct((g, p, n), _f32))
    return pl.pallas_call(_ssm_prep_kernel, out_shape=out_shape, name="ssm_prep")(
        lam_re, lam_im, log_dt.reshape(g, 1),
        jnp.swapaxes(b_re, 1, 2), jnp.swapaxes(b_im, 1, 2))


def _block_diag(blocks):
    g, r, c = blocks.shape
    eye = jnp.eye(g, dtype=blocks.dtype)
    return (blocks[:, :, None, :] * eye[:, None, :, None]).reshape(g * r, g * c)


def _mixer_kernel(x_ref, sh_ref, sc_ref, g1_ref, w_in_ref, b_in_ref, bc_ref, cc_ref, ar_ref, ai_ref,
                  dvec_ref, wa_ref, wb_ref, dww_ref, dwb_ref, cvg_ref, cvb_ref, wpw_ref, wout_ref,
                  ln1g_ref, ln1b_ref, o_ref,
                  h_scr, u_scr, gate_scr, bu_scr, hst_scr, vbuf_scr, cv_scr):
    step = pl.program_id(0)

    @pl.when(step == 0)
    def _():
        hst_scr[...] = jnp.zeros_like(hst_scr)
        vbuf_scr[0:HIST31, :] = jnp.zeros((HIST31, CONV_WIDTH), _f32)

    h_scr[...] = _modulated_ln(x_ref[...], sh_ref[...], sc_ref[...]).astype(_bf16)

    o1 = SSM_WIDTH
    o2 = o1 + CONV_WIDTH
    o3 = o2 + CONV_WIDTH
    u_scr[...] = _mm(h_scr[...], w_in_ref[:, 0:o1]) + b_in_ref[:, 0:o1]
    cv_a = _mm(h_scr[...], w_in_ref[:, o1:o2]) + b_in_ref[:, o1:o2]
    cv_g = _mm(h_scr[...], w_in_ref[:, o2:o3]) + b_in_ref[:, o2:o3]
    vbuf_scr[HIST31:HIST31 + ROWS, :] = cv_a * jax.nn.sigmoid(cv_g)
    for j in range(2 * D_MODEL // 512):
        lo = o3 + j * 512
        gate_scr[:, j * 512:(j + 1) * 512] = jax.nn.sigmoid(
            _mm(h_scr[...], w_in_ref[:, lo:lo + 512]) + b_in_ref[:, lo:lo + 512])

    n_chunks = N_STATE // V7X_MXU_DIM
    for part in range(2):
        for j in range(n_chunks):
            kin = (j // 2) * V7X_LANES
            col = part * N_STATE + j * V7X_MXU_DIM
            bu_scr[:, col:col + V7X_MXU_DIM] = _mm(u_scr[:, kin:kin + V7X_LANES],
                                                  bc_ref[part * n_chunks + j])

    for cb in range(N_STATE // SCAN_LANES):
        re_cols = pl.ds(cb * SCAN_LANES, SCAN_LANES)
        im_cols = pl.ds(N_STATE + cb * SCAN_LANES, SCAN_LANES)
        a_r = ar_ref[:, re_cols]
        a_i = ai_ref[:, re_cols]

        def scan_step(t, carry, re_cols=re_cols, im_cols=im_cols, a_r=a_r, a_i=a_i):
            h_r, h_i = carry
            rows = pl.ds(pl.multiple_of(t * BATCH, BATCH), BATCH)
            n_r = a_r * h_r - a_i * h_i + bu_scr[rows, re_cols]
            n_i = a_r * h_i + a_i * h_r + bu_scr[rows, im_cols]
            bu_scr[rows, re_cols] = n_r
            bu_scr[rows, im_cols] = n_i
            return n_r, n_i

        h_r, h_i = lax.fori_loop(0, TIME_TILE, scan_step,
                                 (hst_scr[:, re_cols], hst_scr[:, im_cols]), unroll=8)
        hst_scr[:, re_cols] = h_r
        hst_scr[:, im_cols] = h_i

    k_per_out = (N_STATE // V7X_MXU_DIM) // (SSM_WIDTH // V7X_MXU_DIM)
    y_parts = []
    for oc in range(SSM_WIDTH // V7X_MXU_DIM):
        acc = dvec_ref[:, oc * V7X_MXU_DIM:(oc + 1) * V7X_MXU_DIM] * u_scr[:, oc * V7X_MXU_DIM:(oc + 1) * V7X_MXU_DIM]
        for part in range(2):
            for k in range(k_per_out):
                kk = oc * k_per_out + k
                col = part * N_STATE + kk * V7X_MXU_DIM
                acc = acc + _mm(bu_scr[:, col:col + V7X_MXU_DIM], cc_ref[part * n_chunks + kk])
        y_parts.append(_gelu(acc).astype(_bf16))
    y_act = jnp.concatenate(y_parts, axis=1)
    y_ssm = _mm(y_act, wa_ref[...]) * jax.nn.sigmoid(_mm(y_act, wb_ref[...]))
    merged = gate_scr[:, 0:D_MODEL] * y_ssm

    def conv_block(r, carry):
        base = pl.multiple_of(r * CONV_ROWS, CONV_ROWS)
        acc = jnp.broadcast_to(dwb_ref[...], (CONV_ROWS, CONV_WIDTH))
        for k in range(CONV_KERNEL):
            acc = acc + dww_ref[k:k + 1, :] * vbuf_scr[pl.ds(base + k * BATCH, CONV_ROWS), :]
        cv_scr[pl.ds(base, CONV_ROWS), :] = acc
        return carry

    lax.fori_loop(0, ROWS // CONV_ROWS, conv_block, 0)
    vbuf_scr[0:HIST31, :] = vbuf_scr[ROWS:ROWS + HIST31, :]
    cv = _ln(cv_scr[...]) * cvg_ref[...] + cvb_ref[...]
    cv = cv * jax.nn.sigmoid(cv)
    y_cv = _mm(cv, wpw_ref[...])
    merged = merged + gate_scr[:, D_MODEL:2 * D_MODEL] * y_cv

    y = _mm(merged, wout_ref[...])
    o_ref[...] = _residual_ln(x_ref[...], y, g1_ref[...], ln1g_ref[...], ln1b_ref[...])


def _vmem_full():
    return pl.BlockSpec(memory_space=pltpu.VMEM)


def _mixer(x_tm, sh, sc, g1, w_in, b_in, bc, cc, a_r, a_i, dvec, wa, wb, dww, dwb, cvg, cvb, wpw, wout,
           ln1g, ln1b):
    n_rows = x_tm.shape[0]
    row_spec = pl.BlockSpec((ROWS, D_MODEL), lambda i: (i, 0))
    return pl.pallas_call(
        _mixer_kernel,
        grid=(n_rows // ROWS,),
        in_specs=[row_spec] + [_vmem_full()] * 20,
        out_specs=row_spec,
        out_shape=jax.ShapeDtypeStruct((n_rows, D_MODEL), _f32),
        scratch_shapes=[
            pltpu.VMEM((ROWS, D_MODEL), _bf16),
            pltpu.VMEM((ROWS, SSM_WIDTH), _f32),
            pltpu.VMEM((ROWS, 2 * D_MODEL), _f32),
            pltpu.VMEM((ROWS, 2 * N_STATE), _f32),
            pltpu.VMEM((BATCH, 2 * N_STATE), _f32),
            pltpu.VMEM((HIST31 + ROWS, CONV_WIDTH), _f32),
            pltpu.VMEM((ROWS, CONV_WIDTH), _f32),
        ],
        compiler_params=pltpu.CompilerParams(dimension_semantics=("arbitrary",),
                                             vmem_limit_bytes=VMEM_LIMIT),
        name="mixer",
    )(x_tm, sh, sc, g1, w_in, b_in, bc, cc, a_r, a_i, dvec, wa, wb, dww, dwb, cvg, cvb, wpw, wout,
      ln1g, ln1b)


def _ffn_kernel(x_ref, sh_ref, sc_ref, g2_ref, wup_ref, dww_ref, dwb_ref, wdn_ref, ln2g_ref, ln2b_ref,
                o_ref, h_scr, ua_scr, uv_scr, hist_scr, act_scr):
    step = pl.program_id(0)

    @pl.when(step == 0)
    def _():
        hist_scr[...] = jnp.zeros_like(hist_scr)

    h_scr[...] = _modulated_ln(x_ref[...], sh_ref[...], sc_ref[...]).astype(_bf16)

    def conv3(buf, col):
        acc = dwb_ref[:, col:col + FFN_CHUNK] + dww_ref[0:1, col:col + FFN_CHUNK] * buf[0:ROWS, :]
        acc = acc + dww_ref[1:2, col:col + FFN_CHUNK] * buf[BATCH:BATCH + ROWS, :]
        return acc + dww_ref[2:3, col:col + FFN_CHUNK] * buf[HIST3:HIST3 + ROWS, :]

    for c in range(FFN_HIDDEN // FFN_CHUNK):
        ca = c * FFN_CHUNK
        cv = FFN_HIDDEN + c * FFN_CHUNK
        ua_scr[0:HIST3, :] = hist_scr[:, ca:ca + FFN_CHUNK]
        uv_scr[0:HIST3, :] = hist_scr[:, cv:cv + FFN_CHUNK]
        ua_scr[HIST3:HIST3 + ROWS, :] = _mm(h_scr[...], wup_ref[:, ca:ca + FFN_CHUNK])
        uv_scr[HIST3:HIST3 + ROWS, :] = _mm(h_scr[...], wup_ref[:, cv:cv + FFN_CHUNK])
        hist_scr[:, ca:ca + FFN_CHUNK] = ua_scr[ROWS:ROWS + HIST3, :]
        hist_scr[:, cv:cv + FFN_CHUNK] = uv_scr[ROWS:ROWS + HIST3, :]
        act_scr[:, ca:ca + FFN_CHUNK] = (_gelu(conv3(ua_scr, ca)) * conv3(uv_scr, cv)).astype(_bf16)

    y = _mm(act_scr[...], wdn_ref[...])
    o_ref[...] = _residual_ln(x_ref[...], y, g2_ref[...], ln2g_ref[...], ln2b_ref[...])


def _ffn(x_tm, sh, sc, g2, wup, dww, dwb, wdn, ln2g, ln2b):
    n_rows = x_tm.shape[0]
    row_spec = pl.BlockSpec((ROWS, D_MODEL), lambda i: (i, 0))
    return pl.pallas_call(
        _ffn_kernel,
        grid=(n_rows // ROWS,),
        in_specs=[row_spec] + [_vmem_full()] * 9,
        out_specs=row_spec,
        out_shape=jax.ShapeDtypeStruct((n_rows, D_MODEL), _f32),
        scratch_shapes=[
            pltpu.VMEM((ROWS, D_MODEL), _bf16),
            pltpu.VMEM((HIST3 + ROWS, FFN_CHUNK), _f32),
            pltpu.VMEM((HIST3 + ROWS, FFN_CHUNK), _f32),
            pltpu.VMEM((HIST3, 2 * FFN_HIDDEN), _f32),
            pltpu.VMEM((ROWS, FFN_HIDDEN), _bf16),
        ],
        compiler_params=pltpu.CompilerParams(dimension_semantics=("arbitrary",),
                                             vmem_limit_bytes=VMEM_LIMIT),
        name="ffn",
    )(x_tm, sh, sc, g2, wup, dww, dwb, wdn, ln2g, ln2b)


def _layer(x_tm, c_act_mod, w_in, b_in, lam_re, lam_im, log_dt, b_re, b_im, c_re, c_im, d, glu_w_a, glu_w_b,
           cv_dw_w, cv_dw_b, cv_ln_g, cv_ln_b, cv_w_pw, w_out, ln1_g, ln1_b,
           ffn_w_up, ffn_dw_w, ffn_dw_b, ffn_w_down, ln2_g, ln2_b):
    sh1, sc1, g1, sh2, sc2, g2 = jnp.split(c_act_mod, N_COND, axis=-1)

    a_r, a_i, bbar_r, bbar_i = _ssm_prep(lam_re, lam_im, log_dt, b_re, b_im)
    n_chunks = N_STATE // V7X_MXU_DIM

    def b_chunks(bbar):
        full = _block_diag(bbar)
        return jnp.stack([full[(j // 2) * V7X_LANES:(j // 2 + 1) * V7X_LANES,
                               j * V7X_MXU_DIM:(j + 1) * V7X_MXU_DIM] for j in range(n_chunks)])

    def c_chunks(cmat):
        full = _block_diag(jnp.swapaxes(cmat, 1, 2))
        k_per_out = n_chunks // (SSM_WIDTH // V7X_MXU_DIM)
        return jnp.stack([full[k * V7X_MXU_DIM:(k + 1) * V7X_MXU_DIM,
                               (k // k_per_out) * V7X_MXU_DIM:(k // k_per_out + 1) * V7X_MXU_DIM]
                          for k in range(n_chunks)])

    bc = jnp.concatenate([b_chunks(bbar_r), b_chunks(bbar_i)]).astype(_bf16)
    cc = jnp.concatenate([c_chunks(c_re), -c_chunks(c_im)]).astype(_bf16)
    a_r8 = jnp.broadcast_to(a_r.reshape(1, N_STATE), (BATCH, N_STATE))
    a_i8 = jnp.broadcast_to(a_i.reshape(1, N_STATE), (BATCH, N_STATE))

    row = lambda v: v.reshape(1, -1)
    x1 = _mixer(x_tm, sh1, sc1, g1, w_in.astype(_bf16), row(b_in), bc, cc, a_r8, a_i8, row(d),
                glu_w_a.astype(_bf16), glu_w_b.astype(_bf16),
                cv_dw_w.reshape(CONV_KERNEL, CONV_WIDTH), row(cv_dw_b), row(cv_ln_g), row(cv_ln_b),
                cv_w_pw.astype(_bf16), w_out.astype(_bf16), row(ln1_g), row(ln1_b))
    return _ffn(x1, sh2, sc2, g2, ffn_w_up.astype(_bf16), ffn_dw_w.reshape(FFN_KERNEL, 2 * FFN_HIDDEN),
                row(ffn_dw_b), ffn_w_down.astype(_bf16), row(ln2_g), row(ln2_b))


def kernel(x, c, w_cond, b_cond, w_in, b_in, ssm_lambda_re, ssm_lambda_im, ssm_log_dt, ssm_b_re, ssm_b_im, ssm_c_re, ssm_c_im, ssm_d, ssm_glu_w_a, ssm_glu_w_b, cv_dw_w, cv_dw_b, cv_ln_g, cv_ln_b, cv_w_pw, w_out, ln1_g, ln1_b, ffn_w_up, ffn_dw_w, ffn_dw_b, ffn_w_down, ln2_g, ln2_b):
    bsz, seq, dm = x.shape
    depth = w_cond.shape[0]
    x_tm = jnp.swapaxes(x, 0, 1).reshape(seq * bsz, dm)
    for l in range(depth):
        mod = _cond(c, w_cond[l], b_cond[l])
        x_tm = _layer(x_tm, mod, w_in[l], b_in[l], ssm_lambda_re[l], ssm_lambda_im[l], ssm_log_dt[l],
                      ssm_b_re[l], ssm_b_im[l], ssm_c_re[l], ssm_c_im[l], ssm_d[l],
                      ssm_glu_w_a[l], ssm_glu_w_b[l], cv_dw_w[l], cv_dw_b[l], cv_ln_g[l], cv_ln_b[l],
                      cv_w_pw[l], w_out[l], ln1_g[l], ln1_b[l],
                      ffn_w_up[l], ffn_dw_w[l], ffn_dw_b[l], ffn_w_down[l], ln2_g[l], ln2_b[l])
    return jnp.swapaxes(x_tm.reshape(seq, bsz, dm), 0, 1)
```

```python
import functools

import jax
import jax.numpy as jnp
from jax import lax
from jax.experimental import pallas as pl
from jax.experimental.pallas import tpu as pltpu

D_MODEL = 1024
BATCH = 8
SSM_WIDTH = 512
SSM_GROUP = 16
SSM_GROUPS = 32
SSM_STATE = 64
N_STATE = SSM_GROUPS * SSM_STATE
CONV_WIDTH = 512
CONV_KERNEL = 31
FFN_HIDDEN = 2816
FFN_KERNEL = 3
N_COND = 6
IN_PROJ_WIDTH = SSM_WIDTH + 2 * CONV_WIDTH + 2 * D_MODEL
DEEPNORM_ALPHA = 2.0 ** 0.25
LN_EPS = 1e-5

V7X_SUBLANES = 8
V7X_LANES = 128
V7X_MXU_DIM = 256
V7X_VMEM_BYTES = 64 * 1024 * 1024

TIME_TILE = 64
ROWS = TIME_TILE * BATCH
SCAN_LANES = 512
FFN_CHUNK = 256
CONV_ROWS = 64
HIST31 = (CONV_KERNEL - 1) * BATCH
HIST3 = (FFN_KERNEL - 1) * BATCH
VMEM_LIMIT = V7X_VMEM_BYTES - 8 * 1024 * 1024

_bf16 = jnp.bfloat16
_f32 = jnp.float32


def _mm(a, b):
    return jnp.dot(a.astype(_bf16), b.astype(_bf16), preferred_element_type=_f32)


def _ln(x):
    mu = jnp.mean(x, axis=-1, keepdims=True)
    xc = x - mu
    var = jnp.mean(xc * xc, axis=-1, keepdims=True)
    return xc * lax.rsqrt(var + LN_EPS)


def _gelu(x):
    return 0.5 * x * (1.0 + lax.erf(x * (2.0 ** -0.5)))


def _per_batch(v, rows):
    return v[None]


def _modulated_ln(x, shift, scale):
    rows, n = x.shape
    y = _ln(x).reshape(rows // BATCH, BATCH, n)
    y = y * (1.0 + scale[None]) + shift[None]
    return y.reshape(rows, n)


def _residual_ln(x, y, gate, g, b):
    rows, n = x.shape
    z = DEEPNORM_ALPHA * x.reshape(rows // BATCH, BATCH, n) + gate[None] * y.reshape(rows // BATCH, BATCH, n)
    return _ln(z.reshape(rows, n)) * g + b


def _tile_copies(hbm_ref, vmem_tile, sems, tile, to_vmem):
    copies = []
    for b in range(BATCH):
        hbm = hbm_ref.at[b, pl.ds(tile * TIME_TILE, TIME_TILE), :]
        vmem = vmem_tile.at[:, b, :]
        src, dst = (hbm, vmem) if to_vmem else (vmem, hbm)
        copies.append(pltpu.make_async_copy(src, dst, sems.at[b]))
    return copies


def _cond_kernel(c_ref, w_ref, b_ref, o_ref):
    c = c_ref[...]
    c_act = c * jax.nn.sigmoid(c)
    o_ref[...] = jnp.dot(c_act, w_ref[...], preferred_element_type=_f32,
                         precision=lax.Precision.HIGHEST) + b_ref[...]


def _cond(c, w, b):
    n = w.shape[1]
    bn = 1536
    return pl.pallas_call(
        _cond_kernel,
        grid=(n // bn,),
        in_specs=[pl.BlockSpec((BATCH, D_MODEL), lambda j: (0, 0)),
                  pl.BlockSpec((D_MODEL, bn), lambda j: (0, j)),
                  pl.BlockSpec((1, bn), lambda j: (0, j))],
        out_specs=pl.BlockSpec((BATCH, bn), lambda j: (0, j)),
        out_shape=jax.ShapeDtypeStruct((BATCH, n), _f32),
        name="cond",
    )(c, w, b.reshape(1, n))


def _ssm_prep_kernel(lre_ref, lim_ref, ldt_ref, bre_ref, bim_ref, ar_ref, ai_ref, bbr_ref, bbi_ref):
    lr = jnp.minimum(lre_ref[...], -1e-4)
    li = lim_ref[...]
    dt = jnp.exp(ldt_ref[...])
    mag = jnp.exp(lr * dt)
    ang = li * dt
    lbr = mag * jnp.cos(ang)
    lbi = mag * jnp.sin(ang)
    num_r = lbr - 1.0
    num_i = lbi
    den = lr * lr + li * li
    coef_r = (num_r * lr + num_i * li) / den
    coef_i = (num_i * lr - num_r * li) / den
    ar_ref[...] = lbr
    ai_ref[...] = lbi
    br = bre_ref[...]
    bi = bim_ref[...]
    bbr_ref[...] = coef_r[:, None, :] * br - coef_i[:, None, :] * bi
    bbi_ref[...] = coef_r[:, None, :] * bi + coef_i[:, None, :] * br


def _ssm_prep(lam_re, lam_im, log_dt, b_re, b_im):
    g, n, p = b_re.shape
    out_shape = (jax.ShapeDtypeStruct((g, n), _f32), jax.ShapeDtypeStruct((g, n), _f32),
                 jax.ShapeDtypeStruct((g, p, n), _f32), jax.ShapeDtypeStruct((g, p, n), _f32))
    return pl.pallas_call(_ssm_prep_kernel, out_shape=out_shape, name="ssm_prep")(
        lam_re, lam_im, log_dt.reshape(g, 1),
        jnp.swapaxes(b_re, 1, 2), jnp.swapaxes(b_im, 1, 2))


def _block_diag(blocks):
    g, r, c = blocks.shape
    eye = jnp.eye(g, dtype=blocks.dtype)
    return (blocks[:, :, None, :] * eye[:, None, :, None]).reshape(g * r, g * c)


def _mixer_kernel(x_ref, sh_ref, sc_ref, g1_ref, w_in_ref, b_in_ref, bc_ref, cc_ref, ar_ref, ai_ref,
                  dvec_ref, wa_ref, wb_ref, dww_ref, dwb_ref, cvg_ref, cvb_ref, wpw_ref, wout_ref,
                  ln1g_ref, ln1b_ref, o_ref,
                  xbuf, xsem, h_scr, u_scr, gate_scr, bu_scr, hst_scr, vbuf_scr, cv_scr):
    step = pl.program_id(0)
    n_steps = pl.num_programs(0)
    slot = step % 2

    @pl.when(step == 0)
    def _():
        for cp in _tile_copies(x_ref, xbuf.at[0], xsem.at[0], 0, True):
            cp.start()
        hst_scr[...] = jnp.zeros_like(hst_scr)
        vbuf_scr[0:HIST31, :] = jnp.zeros((HIST31, CONV_WIDTH), _f32)

    @pl.when(step + 1 < n_steps)
    def _():
        for cp in _tile_copies(x_ref, xbuf.at[1 - slot], xsem.at[1 - slot], step + 1, True):
            cp.start()

    for cp in _tile_copies(x_ref, xbuf.at[slot], xsem.at[slot], step, True):
        cp.wait()
    x_tile = xbuf.at[slot]

    h_scr[...] = _modulated_ln(x_tile[...].reshape(ROWS, D_MODEL), sh_ref[...], sc_ref[...]).astype(_bf16)

    o1 = SSM_WIDTH
    o2 = o1 + CONV_WIDTH
    o3 = o2 + CONV_WIDTH
    u_scr[...] = _mm(h_scr[...], w_in_ref[:, 0:o1]) + b_in_ref[:, 0:o1]
    cv_a = _mm(h_scr[...], w_in_ref[:, o1:o2]) + b_in_ref[:, o1:o2]
    cv_g = _mm(h_scr[...], w_in_ref[:, o2:o3]) + b_in_ref[:, o2:o3]
    vbuf_scr[HIST31:HIST31 + ROWS, :] = cv_a * jax.nn.sigmoid(cv_g)
    for j in range(2 * D_MODEL // 512):
        lo = o3 + j * 512
        gate_scr[:, j * 512:(j + 1) * 512] = jax.nn.sigmoid(
            _mm(h_scr[...], w_in_ref[:, lo:lo + 512]) + b_in_ref[:, lo:lo + 512])

    n_chunks = N_STATE // V7X_MXU_DIM
    for part in range(2):
        for j in range(n_chunks):
            kin = (j // 2) * V7X_LANES
            col = part * N_STATE + j * V7X_MXU_DIM
            bu_scr[:, col:col + V7X_MXU_DIM] = _mm(u_scr[:, kin:kin + V7X_LANES],
                                                  bc_ref[part * n_chunks + j])

    for cb in range(N_STATE // SCAN_LANES):
        re_cols = pl.ds(cb * SCAN_LANES, SCAN_LANES)
        im_cols = pl.ds(N_STATE + cb * SCAN_LANES, SCAN_LANES)
        a_r = ar_ref[:, re_cols]
        a_i = ai_ref[:, re_cols]
        h_r = hst_scr[:, re_cols]
        h_i = hst_scr[:, im_cols]
        for t in range(TIME_TILE):
            rows = pl.ds(t * BATCH, BATCH)
            n_r = a_r * h_r - a_i * h_i + bu_scr[rows, re_cols]
            n_i = a_r * h_i + a_i * h_r + bu_scr[rows, im_cols]
            bu_scr[rows, re_cols] = n_r
            bu_scr[rows, im_cols] = n_i
            h_r, h_i = n_r, n_i
        hst_scr[:, re_cols] = h_r
        hst_scr[:, im_cols] = h_i

    k_per_out = (N_STATE // V7X_MXU_DIM) // (SSM_WIDTH // V7X_MXU_DIM)
    y_parts = []
    for oc in range(SSM_WIDTH // V7X_MXU_DIM):
        acc = dvec_ref[:, oc * V7X_MXU_DIM:(oc + 1) * V7X_MXU_DIM] * u_scr[:, oc * V7X_MXU_DIM:(oc + 1) * V7X_MXU_DIM]
        for part in range(2):
            for k in range(k_per_out):
                kk = oc * k_per_out + k
                col = part * N_STATE + kk * V7X_MXU_DIM
                acc = acc + _mm(bu_scr[:, col:col + V7X_MXU_DIM], cc_ref[part * n_chunks + kk])
        y_parts.append(_gelu(acc).astype(_bf16))
    y_act = jnp.concatenate(y_parts, axis=1)
    y_ssm = _mm(y_act, wa_ref[...]) * jax.nn.sigmoid(_mm(y_act, wb_ref[...]))
    merged = gate_scr[:, 0:D_MODEL] * y_ssm

    for lb in range(CONV_WIDTH // V7X_LANES):
        lanes = pl.ds(lb * V7X_LANES, V7X_LANES)
        for rb in range(ROWS // CONV_ROWS):
            base = rb * CONV_ROWS
            win = vbuf_scr[base:base + CONV_ROWS + HIST31, lanes]
            acc = dwb_ref[:, lanes] + dww_ref[0:1, lanes] * win[0:CONV_ROWS]
            for k in range(1, CONV_KERNEL):
                acc = acc + dww_ref[k:k + 1, lanes] * win[k * BATCH:k * BATCH + CONV_ROWS]
            cv_scr[base:base + CONV_ROWS, lanes] = acc
    vbuf_scr[0:HIST31, :] = vbuf_scr[ROWS:ROWS + HIST31, :]
    cv = _ln(cv_scr[...]) * cvg_ref[...] + cvb_ref[...]
    cv = cv * jax.nn.sigmoid(cv)
    y_cv = _mm(cv, wpw_ref[...])
    merged = merged + gate_scr[:, D_MODEL:2 * D_MODEL] * y_cv

    y = _mm(merged, wout_ref[...])
    o_ref[...] = _residual_ln(x_tile[...].reshape(ROWS, D_MODEL), y, g1_ref[...], ln1g_ref[...], ln1b_ref[...])


def _vmem_full():
    return pl.BlockSpec(memory_space=pltpu.VMEM)


def _mixer(x, sh, sc, g1, w_in, b_in, bc, cc, a_r, a_i, dvec, wa, wb, dww, dwb, cvg, cvb, wpw, wout,
           ln1g, ln1b):
    bsz, seq, _ = x.shape
    n_rows = bsz * seq
    return pl.pallas_call(
        _mixer_kernel,
        grid=(n_rows // ROWS,),
        in_specs=[pl.BlockSpec(memory_space=pl.ANY)] + [_vmem_full()] * 20,
        out_specs=pl.BlockSpec((ROWS, D_MODEL), lambda i: (i, 0)),
        out_shape=jax.ShapeDtypeStruct((n_rows, D_MODEL), _f32),
        scratch_shapes=[
            pltpu.VMEM((2, TIME_TILE, BATCH, D_MODEL), _f32),
            pltpu.SemaphoreType.DMA((2, BATCH)),
            pltpu.VMEM((ROWS, D_MODEL), _bf16),
            pltpu.VMEM((ROWS, SSM_WIDTH), _f32),
            pltpu.VMEM((ROWS, 2 * D_MODEL), _f32),
            pltpu.VMEM((ROWS, 2 * N_STATE), _f32),
            pltpu.VMEM((BATCH, 2 * N_STATE), _f32),
            pltpu.VMEM((HIST31 + ROWS, CONV_WIDTH), _f32),
            pltpu.VMEM((ROWS, CONV_WIDTH), _f32),
        ],
        compiler_params=pltpu.CompilerParams(dimension_semantics=("arbitrary",),
                                             vmem_limit_bytes=VMEM_LIMIT),
        name="mixer",
    )(x, sh, sc, g1, w_in, b_in, bc, cc, a_r, a_i, dvec, wa, wb, dww, dwb, cvg, cvb, wpw, wout,
      ln1g, ln1b)


def _ffn_kernel(x_ref, sh_ref, sc_ref, g2_ref, wup_ref, dww_ref, dwb_ref, wdn_ref, ln2g_ref, ln2b_ref,
                o_ref, obuf, osem, h_scr, ua_scr, uv_scr, hist_scr, act_scr):
    step = pl.program_id(0)
    n_steps = pl.num_programs(0)
    slot = step % 2

    @pl.when(step == 0)
    def _():
        hist_scr[...] = jnp.zeros_like(hist_scr)

    h_scr[...] = _modulated_ln(x_ref[...], sh_ref[...], sc_ref[...]).astype(_bf16)

    def conv3(buf, col):
        acc = dwb_ref[:, col:col + FFN_CHUNK] + dww_ref[0:1, col:col + FFN_CHUNK] * buf[0:ROWS, :]
        acc = acc + dww_ref[1:2, col:col + FFN_CHUNK] * buf[BATCH:BATCH + ROWS, :]
        return acc + dww_ref[2:3, col:col + FFN_CHUNK] * buf[HIST3:HIST3 + ROWS, :]

    for c in range(FFN_HIDDEN // FFN_CHUNK):
        ca = c * FFN_CHUNK
        cv = FFN_HIDDEN + c * FFN_CHUNK
        ua_scr[0:HIST3, :] = hist_scr[:, ca:ca + FFN_CHUNK]
        uv_scr[0:HIST3, :] = hist_scr[:, cv:cv + FFN_CHUNK]
        ua_scr[HIST3:HIST3 + ROWS, :] = _mm(h_scr[...], wup_ref[:, ca:ca + FFN_CHUNK])
        uv_scr[HIST3:HIST3 + ROWS, :] = _mm(h_scr[...], wup_ref[:, cv:cv + FFN_CHUNK])
        hist_scr[:, ca:ca + FFN_CHUNK] = ua_scr[ROWS:ROWS + HIST3, :]
        hist_scr[:, cv:cv + FFN_CHUNK] = uv_scr[ROWS:ROWS + HIST3, :]
        act_scr[:, ca:ca + FFN_CHUNK] = (_gelu(conv3(ua_scr, ca)) * conv3(uv_scr, cv)).astype(_bf16)

    y = _mm(act_scr[...], wdn_ref[...])
    out = _residual_ln(x_ref[...], y, g2_ref[...], ln2g_ref[...], ln2b_ref[...])

    @pl.when(step >= 2)
    def _():
        for cp in _tile_copies(o_ref, obuf.at[slot], osem.at[slot], step - 2, False):
            cp.wait()

    obuf[slot] = out.reshape(TIME_TILE, BATCH, D_MODEL)
    for cp in _tile_copies(o_ref, obuf.at[slot], osem.at[slot], step, False):
        cp.start()

    @pl.when(step == n_steps - 1)
    def _():
        for cp in _tile_copies(o_ref, obuf.at[1 - slot], osem.at[1 - slot], step - 1, False):
            cp.wait()
        for cp in _tile_copies(o_ref, obuf.at[slot], osem.at[slot], step, False):
            cp.wait()


def _ffn(x_tm, sh, sc, g2, wup, dww, dwb, wdn, ln2g, ln2b):
    n_rows = x_tm.shape[0]
    seq = n_rows // BATCH
    return pl.pallas_call(
        _ffn_kernel,
        grid=(n_rows // ROWS,),
        in_specs=[pl.BlockSpec((ROWS, D_MODEL), lambda i: (i, 0))] + [_vmem_full()] * 9,
        out_specs=pl.BlockSpec(memory_space=pl.ANY),
        out_shape=jax.ShapeDtypeStruct((BATCH, seq, D_MODEL), _f32),
        scratch_shapes=[
            pltpu.VMEM((2, TIME_TILE, BATCH, D_MODEL), _f32),
            pltpu.SemaphoreType.DMA((2, BATCH)),
            pltpu.VMEM((ROWS, D_MODEL), _bf16),
            pltpu.VMEM((HIST3 + ROWS, FFN_CHUNK), _f32),
            pltpu.VMEM((HIST3 + ROWS, FFN_CHUNK), _f32),
            pltpu.VMEM((HIST3, 2 * FFN_HIDDEN), _f32),
            pltpu.VMEM((ROWS, FFN_HIDDEN), _bf16),
        ],
        compiler_params=pltpu.CompilerParams(dimension_semantics=("arbitrary",),
                                             vmem_limit_bytes=VMEM_LIMIT),
        name="ffn",
    )(x_tm, sh, sc, g2, wup, dww, dwb, wdn, ln2g, ln2b)


def _layer(x, c_act_mod, w_in, b_in, lam_re, lam_im, log_dt, b_re, b_im, c_re, c_im, d, glu_w_a, glu_w_b,
           cv_dw_w, cv_dw_b, cv_ln_g, cv_ln_b, cv_w_pw, w_out, ln1_g, ln1_b,
           ffn_w_up, ffn_dw_w, ffn_dw_b, ffn_w_down, ln2_g, ln2_b):
    sh1, sc1, g1, sh2, sc2, g2 = jnp.split(c_act_mod, N_COND, axis=-1)

    a_r, a_i, bbar_r, bbar_i = _ssm_prep(lam_re, lam_im, log_dt, b_re, b_im)
    n_chunks = N_STATE // V7X_MXU_DIM

    def b_chunks(bbar):
        full = _block_diag(bbar)
        return jnp.stack([full[(j // 2) * V7X_LANES:(j // 2 + 1) * V7X_LANES,
                               j * V7X_MXU_DIM:(j + 1) * V7X_MXU_DIM] for j in range(n_chunks)])

    def c_chunks(cmat):
        full = _block_diag(jnp.swapaxes(cmat, 1, 2))
        k_per_out = n_chunks // (SSM_WIDTH // V7X_MXU_DIM)
        return jnp.stack([full[k * V7X_MXU_DIM:(k + 1) * V7X_MXU_DIM,
                               (k // k_per_out) * V7X_MXU_DIM:(k // k_per_out + 1) * V7X_MXU_DIM]
                          for k in range(n_chunks)])

    bc = jnp.concatenate([b_chunks(bbar_r), b_chunks(bbar_i)]).astype(_bf16)
    cc = jnp.concatenate([c_chunks(c_re), -c_chunks(c_im)]).astype(_bf16)
    a_r8 = jnp.broadcast_to(a_r.reshape(1, N_STATE), (BATCH, N_STATE))
    a_i8 = jnp.broadcast_to(a_i.reshape(1, N_STATE), (BATCH, N_STATE))

    row = lambda v: v.reshape(1, -1)
    x1 = _mixer(x, sh1, sc1, g1, w_in.astype(_bf16), row(b_in), bc, cc, a_r8, a_i8, row(d),
                glu_w_a.astype(_bf16), glu_w_b.astype(_bf16),
                cv_dw_w.reshape(CONV_KERNEL, CONV_WIDTH), row(cv_dw_b), row(cv_ln_g), row(cv_ln_b),
                cv_w_pw.astype(_bf16), w_out.astype(_bf16), row(ln1_g), row(ln1_b))
    return _ffn(x1, sh2, sc2, g2, ffn_w_up.astype(_bf16), ffn_dw_w.reshape(FFN_KERNEL, 2 * FFN_HIDDEN),
                row(ffn_dw_b), ffn_w_down.astype(_bf16), row(ln2_g), row(ln2_b))


def kernel(x, c, w_cond, b_cond, w_in, b_in, ssm_lambda_re, ssm_lambda_im, ssm_log_dt, ssm_b_re, ssm_b_im, ssm_c_re, ssm_c_im, ssm_d, ssm_glu_w_a, ssm_glu_w_b, cv_dw_w, cv_dw_b, cv_ln_g, cv_ln_b, cv_w_pw, w_out, ln1_g, ln1_b, ffn_w_up, ffn_dw_w, ffn_dw_b, ffn_w_down, ln2_g, ln2_b):
    bsz, seq, dm = x.shape
    depth = w_cond.shape[0]
    assert (bsz, dm) == (BATCH, D_MODEL) and seq % TIME_TILE == 0 and seq // TIME_TILE >= 2
    for l in range(depth):
        mod = _cond(c, w_cond[l], b_cond[l])
        x = _layer(x, mod, w_in[l], b_in[l], ssm_lambda_re[l], ssm_lambda_im[l], ssm_log_dt[l],
                      ssm_b_re[l], ssm_b_im[l], ssm_c_re[l], ssm_c_im[l], ssm_d[l],
                      ssm_glu_w_a[l], ssm_glu_w_b[l], cv_dw_w[l], cv_dw_b[l], cv_ln_g[l], cv_ln_b[l],
                      cv_w_pw[l], w_out[l], ln1_g[l], ln1_b[l],
                      ffn_w_up[l], ffn_dw_w[l], ffn_dw_b[l], ffn_w_down[l], ln2_g[l], ln2_b[l])
    return x
```
